```python
import math
import jax, jax.numpy as jnp
from jax import lax
import numpy as np

D_MODEL = 1024
BATCH = 4
SEQ = 8192
DEPTH = 2

PLE_DIM = 256
NORM_EPS = 1e-6
S5_WIDTH = 512
S5_GROUP = 16
S5_GROUPS = S5_WIDTH // S5_GROUP
S5_STATE = 64
S5_DT_MIN = 1e-3
S5_DT_MAX = 1e-1
LRU_WIDTH = 1280
LRU_HEADS = 10
LRU_HEAD_DIM = LRU_WIDTH // LRU_HEADS
LRU_C = 8.0
CONV_WIDTH = 4
IN_SPLITS = (
    S5_WIDTH,
    2 * S5_WIDTH,
    2 * S5_WIDTH + LRU_WIDTH,
    2 * S5_WIDTH + 2 * LRU_WIDTH,
    2 * S5_WIDTH + 2 * LRU_WIDTH + D_MODEL,
)
IN_COLS = 2 * S5_WIDTH + 2 * LRU_WIDTH + 2 * D_MODEL

kernel_name = 'hybrid_s5_rglru_gated_parallel'


def rms_norm(x, g):
    xf = x.astype(jnp.float32)
    y = xf * lax.rsqrt(jnp.mean(xf * xf, axis=-1, keepdims=True) + NORM_EPS)
    return (y * g.astype(jnp.float32)).astype(x.dtype)


def _linear_combine(e1, e2):
    a1, b1 = e1
    a2, b2 = e2
    return (a1 * a2, a2 * b1 + b2)


def s5_ssm(u, a_re, a_im, log_dt, b_re, b_im, c_re, c_im, d_skip):
    f32 = jnp.float32
    bsz, seqlen, _ = u.shape
    uf = u.astype(f32)
    ug = uf.reshape(bsz, seqlen, S5_GROUPS, S5_GROUP)
    lam = lax.complex(a_re.astype(f32), a_im.astype(f32))
    dt = jnp.exp(log_dt.astype(f32))[:, None]
    a_bar = jnp.exp(lam * dt)
    zoh = (a_bar - 1.0) / lam
    b = lax.complex(b_re.astype(f32), b_im.astype(f32))
    b_bar = zoh[..., None] * b
    bu = lax.complex(jnp.einsum('blgc,gnc->lbgn', ug, jnp.real(b_bar)),
                     jnp.einsum('blgc,gnc->lbgn', ug, jnp.imag(b_bar)))
    a_seq = jnp.broadcast_to(a_bar[None, None], (seqlen, 1, S5_GROUPS, S5_STATE))
    _, states = lax.associative_scan(_linear_combine, (a_seq, bu), axis=0)
    y = (jnp.einsum('lbgn,gcn->blgc', jnp.real(states), c_re.astype(f32))
         - jnp.einsum('lbgn,gcn->blgc', jnp.imag(states), c_im.astype(f32)))
    y = y.reshape(bsz, seqlen, S5_WIDTH) + d_skip.astype(f32) * uf
    return y.astype(u.dtype)


def causal_depthwise_conv(x, w, b):
    y = lax.conv_general_dilated(
        x, w[:, None, :].astype(x.dtype), window_strides=(1,),
        padding=[(CONV_WIDTH - 1, 0)],
        dimension_numbers=('NWC', 'WIO', 'NWC'),
        feature_group_count=x.shape[-1])
    return y + b


def rg_lru(x, w_a, b_a, w_x, b_x, lam):
    f32 = jnp.float32
    bsz, seqlen, _ = x.shape
    xf = x.astype(f32)
    xh = xf.reshape(bsz, seqlen, LRU_HEADS, LRU_HEAD_DIM)
    r = jax.nn.sigmoid(jnp.einsum('blhi,hij->blhj', xh, w_a.astype(f32)).reshape(bsz, seqlen, LRU_WIDTH)
                       + b_a.astype(f32))
    i = jax.nn.sigmoid(jnp.einsum('blhi,hij->blhj', xh, w_x.astype(f32)).reshape(bsz, seqlen, LRU_WIDTH)
                       + b_x.astype(f32))
    log_a = -LRU_C * r * jax.nn.softplus(-lam.astype(f32))
    a = jnp.exp(log_a)
    mult = jnp.sqrt(-jnp.expm1(2.0 * log_a))
    _, h = lax.associative_scan(_linear_combine, (a, mult * (i * xf)), axis=1)
    return h.astype(x.dtype)


def setup_inputs(seed: int = 0) -> dict:
    key = jax.random.key(seed)
    ks = jax.random.split(key, 32)
    f32 = jnp.float32

    def nrm(k, shape, scale):
        return jax.random.normal(k, shape, f32) * scale

    L, D, G, N, C, HD = DEPTH, D_MODEL, S5_GROUPS, S5_STATE, S5_GROUP, LRU_HEAD_DIM
    x = nrm(ks[0], (BATCH, SEQ, D), 1.0)
    p = nrm(ks[1], (DEPTH, BATCH, SEQ, PLE_DIM), 1.0)
    g_pre = 1.0 + nrm(ks[2], (L, D), 0.05)
    w_in = nrm(ks[3], (L, D, IN_COLS), D ** -0.5)
    s5_a_re = -0.5 + nrm(ks[4], (L, G, N), 0.01)
    s5_a_im = math.pi * jnp.arange(N, dtype=f32)[None, None, :] + nrm(ks[5], (L, G, N), 0.01)
    s5_log_dt = jax.random.uniform(ks[6], (L, G), f32, math.log(S5_DT_MIN), math.log(S5_DT_MAX))
    s5_b_re = nrm(ks[7], (L, G, N, C), C ** -0.5)
    s5_b_im = nrm(ks[8], (L, G, N, C), C ** -0.5)
    s5_c_re = nrm(ks[9], (L, G, C, N), N ** -0.5)
    s5_c_im = nrm(ks[10], (L, G, C, N), N ** -0.5)
    s5_d = nrm(ks[11], (L, S5_WIDTH), 1.0)
    w_glu = nrm(ks[12], (L, S5_WIDTH, 2 * S5_WIDTH), S5_WIDTH ** -0.5)
    w_bs = nrm(ks[13], (L, S5_WIDTH, D), S5_WIDTH ** -0.5)
    conv_w = nrm(ks[14], (L, CONV_WIDTH, LRU_WIDTH), CONV_WIDTH ** -0.5)
    conv_b = nrm(ks[15], (L, LRU_WIDTH), 0.01)
    lru_w_a = nrm(ks[16], (L, LRU_HEADS, HD, HD), HD ** -0.5)
    lru_b_a = nrm(ks[17], (L, LRU_WIDTH), 0.01)
    lru_w_x = nrm(ks[18], (L, LRU_HEADS, HD, HD), HD ** -0.5)
    lru_b_x = nrm(ks[19], (L, LRU_WIDTH), 0.01)
    a_c = jax.random.uniform(ks[20], (L, LRU_WIDTH), f32, 0.9, 0.999)
    sig = a_c ** (1.0 / LRU_C)
    lru_lambda = jnp.log(sig) - jnp.log1p(-sig)
    w_bl = nrm(ks[21], (L, LRU_WIDTH, D), LRU_WIDTH ** -0.5)
    w_out = nrm(ks[22], (L, D, D), D ** -0.5)
    g_post = 1.0 + nrm(ks[23], (L, D), 0.05)
    w_ple = nrm(ks[24], (L, PLE_DIM, D), PLE_DIM ** -0.5)
    w_ple_gate = nrm(ks[25], (L, D, D), D ** -0.5)
    return {
        'x': x, 'p': p, 'g_pre': g_pre, 'w_in': w_in,
        's5_a_re': s5_a_re, 's5_a_im': s5_a_im, 's5_log_dt': s5_log_dt,
        's5_b_re': s5_b_re, 's5_b_im': s5_b_im, 's5_c_re': s5_c_re, 's5_c_im': s5_c_im,
        's5_d': s5_d, 'w_glu': w_glu, 'w_bs': w_bs,
        'conv_w': conv_w, 'conv_b': conv_b,
        'lru_w_a': lru_w_a, 'lru_b_a': lru_b_a, 'lru_w_x': lru_w_x, 'lru_b_x': lru_b_x,
        'lru_lambda': lru_lambda, 'w_bl': w_bl, 'w_out': w_out, 'g_post': g_post,
        'w_ple': w_ple, 'w_ple_gate': w_ple_gate,
    }


def reference(x, p, g_pre, w_in, s5_a_re, s5_a_im, s5_log_dt, s5_b_re, s5_b_im,
              s5_c_re, s5_c_im, s5_d, w_glu, w_bs, conv_w, conv_b,
              lru_w_a, lru_b_a, lru_w_x, lru_b_x, lru_lambda, w_bl, w_out, g_post,
              w_ple, w_ple_gate):
    for i in range(DEPTH):
        h = rms_norm(x, g_pre[i])
        proj = h @ w_in[i]
        s5_x, s5_g, lru_x, lru_g, gate_s, gate_l = jnp.split(proj, IN_SPLITS, axis=-1)

        y_s = s5_ssm(s5_x, s5_a_re[i], s5_a_im[i], s5_log_dt[i], s5_b_re[i], s5_b_im[i],
                     s5_c_re[i], s5_c_im[i], s5_d[i])
        glu_a, glu_b = jnp.split(jax.nn.gelu(y_s) @ w_glu[i], 2, axis=-1)
        y_s = glu_a * jax.nn.sigmoid(glu_b) * jax.nn.silu(s5_g)
        z_s = y_s @ w_bs[i]

        c = causal_depthwise_conv(lru_x, conv_w[i], conv_b[i])
        y_l = rg_lru(c, lru_w_a[i], lru_b_a[i], lru_w_x[i], lru_b_x[i], lru_lambda[i])
        z_l = (y_l * jax.nn.silu(lru_g)) @ w_bl[i]

        merged = jax.nn.sigmoid(gate_s) * z_s + jax.nn.sigmoid(gate_l) * z_l
        x = x + rms_norm(merged @ w_out[i], g_post[i])

        x = x + (p[i] @ w_ple[i]) * jax.nn.sigmoid(x @ w_ple_gate[i])
    return x
```

```python
import functools
import math

import jax
import jax.numpy as jnp
from jax import lax
from jax.experimental import pallas as pl
from jax.experimental.pallas import tpu as pltpu

D_MODEL = 1024
PLE_DIM = 256
NORM_EPS = 1e-6
S5_WIDTH = 512
S5_GROUP = 16
S5_GROUPS = S5_WIDTH // S5_GROUP
S5_STATE = 64
NSTATE = S5_GROUPS * S5_STATE
LRU_WIDTH = 1280
LRU_HEADS = 10
LRU_HEAD_DIM = LRU_WIDTH // LRU_HEADS
LRU_C = 8.0
CONV_WIDTH = 4
IN_COLS = 2 * S5_WIDTH + 2 * LRU_WIDTH + 2 * D_MODEL

C_S5X = 0
C_S5G = S5_WIDTH
C_LRUX = 2 * S5_WIDTH
C_LRUG = C_LRUX + LRU_WIDTH
C_GS = C_LRUG + LRU_WIDTH
C_GL = C_GS + D_MODEL

SUBLANES = 8
LANES = 128
NSEG = SUBLANES
TM = 256
S = TM // NSEG
NSLAB = D_MODEL // LANES
STATE_TILES = NSTATE // LANES
HALO = (CONV_WIDTH - 1) * SUBLANES
VMEM_LIMIT_BYTES = 60 * 1024 * 1024

F32 = jnp.float32
BF16 = jnp.bfloat16


def _dot(a, b):
    return jnp.dot(a, b, preferred_element_type=F32)


def _shift_rows(x, d, fill):
    rolled = pltpu.roll(x, d, axis=0)
    row = lax.broadcasted_iota(jnp.int32, x.shape, 0)
    return jnp.where(row >= d, rolled, fill)


def _bcast_row(ref, r, lo, hi):
    return jnp.broadcast_to(ref[r:r + 1, lo:hi], (SUBLANES, hi - lo))


def _layer_kernel(x_ref, p_ref, gpre_ref, win_ref, bpad_ref, cpad_ref, avec_ref,
                  tpow_ref, tseg_ref, tstep_ref, s5d_ref, wglu_ref, wbs_ref,
                  convw_ref, convb_ref, wgate_ref, ba_ref, bx_ref, lam_ref,
                  wbl_ref, wout_ref, gpost_ref, wple_ref, wpleg_ref,
                  o_ref,
                  xs_ref, hbf_ref, bu_ref, sbf_ref, ext_ref, a_ref, b_ref,
                  xn_ref, cs5_ref, clru_ref, hist_ref):
    i = pl.program_id(1)

    @pl.when(i == 0)
    def _():
        cs5_ref[...] = jnp.zeros_like(cs5_ref)
        clru_ref[...] = jnp.zeros_like(clru_ref)
        hist_ref[...] = jnp.zeros_like(hist_ref)

    blocks_per_seg = S // SUBLANES
    for nb in range(TM // SUBLANES):
        seg, m = divmod(nb, blocks_per_seg)
        for j in range(NSLAB):
            xs_ref[j, pl.ds(SUBLANES * NSEG * m + seg, SUBLANES, stride=NSEG), :] = (
                x_ref[0, SUBLANES * nb:SUBLANES * (nb + 1), LANES * j:LANES * (j + 1)])

    ss = jnp.zeros((TM, LANES), F32)
    for j in range(NSLAB):
        v = xs_ref[j]
        ss = ss + v * v
    inv = lax.rsqrt(jnp.sum(ss, axis=-1, keepdims=True) * (1.0 / D_MODEL) + NORM_EPS)
    for j in range(NSLAB):
        hbf_ref[:, LANES * j:LANES * (j + 1)] = (
            xs_ref[j] * inv * gpre_ref[:, LANES * j:LANES * (j + 1)]).astype(BF16)

    u = _dot(hbf_ref[...], win_ref[:, C_S5X:C_S5X + S5_WIDTH])
    u_bf = u.astype(BF16)
    for j in range(STATE_TILES):
        lt = j // 4
        bu_j = _dot(u_bf[:, LANES * lt:LANES * (lt + 1)], bpad_ref[j])
        bu_ref[:, LANES * j:LANES * (j + 1)] = bu_j[:, :LANES]
        bu_ref[:, NSTATE + LANES * j:NSTATE + LANES * (j + 1)] = bu_j[:, LANES:]

    GW = 512
    ends_re = []
    ends_im = []
    for lg in range(NSTATE // GW):
        lo, hi = GW * lg, GW * (lg + 1)
        ar = avec_ref[:, lo:hi]
        ai = avec_ref[:, NSTATE + lo:NSTATE + hi]

        def s5_step(k, carry, lo=lo, hi=hi, ar=ar, ai=ai):
            sr, si = carry
            r0 = pl.multiple_of(k * SUBLANES, SUBLANES)
            br = bu_ref[pl.ds(r0, SUBLANES), lo:hi]
            bi = bu_ref[pl.ds(r0, SUBLANES), NSTATE + lo:NSTATE + hi]
            nr = ar * sr - ai * si + br
            ni = ar * si + ai * sr + bi
            bu_ref[pl.ds(r0, SUBLANES), lo:hi] = nr
            bu_ref[pl.ds(r0, SUBLANES), NSTATE + lo:NSTATE + hi] = ni
            return nr, ni

        z = jnp.zeros((SUBLANES, GW), F32)
        er, ei = lax.fori_loop(0, S, s5_step, (z, z), unroll=4)
        ends_re.append(er)
        ends_im.append(ei)
    e_re = jnp.concatenate(ends_re, axis=1)
    e_im = jnp.concatenate(ends_im, axis=1)

    def cmul(ar, ai, xr, xi):
        return ar * xr - ai * xi, ar * xi + ai * xr

    ir, ii = e_re, e_im
    for step, d in enumerate((1, 2, 4)):
        pr = _bcast_row(tstep_ref, step, 0, NSTATE)
        pi = _bcast_row(tstep_ref, step, NSTATE, 2 * NSTATE)
        mr, mi = cmul(pr, pi, _shift_rows(ir, d, 0.0), _shift_rows(ii, d, 0.0))
        ir, ii = ir + mr, ii + mi
    cin_r = cs5_ref[:, :NSTATE]
    cin_i = cs5_ref[:, NSTATE:]
    qr, qi = cmul(tseg_ref[:, :NSTATE], tseg_ref[:, NSTATE:], cin_r, cin_i)
    car_r = _shift_rows(ir, 1, 0.0) + qr
    car_i = _shift_rows(ii, 1, 0.0) + qi
    pr = _bcast_row(tstep_ref, 3, 0, NSTATE)
    pi = _bcast_row(tstep_ref, 3, NSTATE, 2 * NSTATE)
    wr, wi = cmul(pr, pi, cin_r, cin_i)
    out_r = ir + wr
    out_i = ii + wi
    cs5_ref[:, :NSTATE] = jnp.broadcast_to(out_r[NSEG - 1:NSEG, :], (SUBLANES, NSTATE))
    cs5_ref[:, NSTATE:] = jnp.broadcast_to(out_i[NSEG - 1:NSEG, :], (SUBLANES, NSTATE))

    for kk in range(S // 2):
        r0 = 2 * SUBLANES * kk
        pr = jnp.concatenate([_bcast_row(tpow_ref, 2 * kk, 0, NSTATE),
                              _bcast_row(tpow_ref, 2 * kk + 1, 0, NSTATE)], axis=0)
        pi = jnp.concatenate([_bcast_row(tpow_ref, 2 * kk, NSTATE, 2 * NSTATE),
                              _bcast_row(tpow_ref, 2 * kk + 1, NSTATE, 2 * NSTATE)], axis=0)
        cr2 = jnp.concatenate([car_r, car_r], axis=0)
        ci2 = jnp.concatenate([car_i, car_i], axis=0)
        fr, fi = cmul(pr, pi, cr2, ci2)
        sbf_ref[r0:r0 + 2 * SUBLANES, :NSTATE] = (bu_ref[r0:r0 + 2 * SUBLANES, :NSTATE] + fr).astype(BF16)
        sbf_ref[r0:r0 + 2 * SUBLANES, NSTATE:] = (bu_ref[r0:r0 + 2 * SUBLANES, NSTATE:] + fi).astype(BF16)

    y_tiles = []
    for lt in range(S5_WIDTH // LANES):
        acc = None
        for j in range(4 * lt, 4 * lt + 4):
            lhs = jnp.concatenate([sbf_ref[:, LANES * j:LANES * (j + 1)],
                                   sbf_ref[:, NSTATE + LANES * j:NSTATE + LANES * (j + 1)]], axis=1)
            t = _dot(lhs, cpad_ref[j])
            acc = t if acc is None else acc + t
        y_tiles.append(acc)
    y_s = jnp.concatenate(y_tiles, axis=1) + s5d_ref[...] * u
    glu = _dot(jax.nn.gelu(y_s).astype(BF16), wglu_ref[...])
    s5_g = _dot(hbf_ref[...], win_ref[:, C_S5G:C_S5G + S5_WIDTH])
    y2 = glu[:, :S5_WIDTH] * jax.nn.sigmoid(glu[:, S5_WIDTH:]) * jax.nn.silu(s5_g)
    z_s = _dot(y2.astype(BF16), wbs_ref[...])

    ext_ref[HALO:, :] = _dot(hbf_ref[...], win_ref[:, C_LRUX:C_LRUX + LRU_WIDTH])
    row8 = lax.broadcasted_iota(jnp.int32, (SUBLANES, LRU_WIDTH), 0)
    for d in range(CONV_WIDTH - 1):
        cur = ext_ref[HALO + TM - HALO + SUBLANES * d:HALO + TM - HALO + SUBLANES * (d + 1), :]
        old = hist_ref[SUBLANES * d:SUBLANES * (d + 1), :]
        ext_ref[SUBLANES * d:SUBLANES * (d + 1), :] = jnp.where(
            row8 == 0, pltpu.roll(old, 1, axis=0), pltpu.roll(cur, 1, axis=0))
        hist_ref[SUBLANES * d:SUBLANES * (d + 1), :] = cur
    conv = convb_ref[...] + convw_ref[CONV_WIDTH - 1:CONV_WIDTH, :] * ext_ref[HALO:HALO + TM, :]
    for t in range(1, CONV_WIDTH):
        conv = conv + (convw_ref[CONV_WIDTH - 1 - t:CONV_WIDTH - t, :]
                       * ext_ref[HALO - SUBLANES * t:HALO - SUBLANES * t + TM, :])
    conv_bf = conv.astype(BF16)
    decay = -LRU_C * jax.nn.softplus(-lam_ref[...])
    for hd in range(LRU_HEADS):
        lo, hi = LRU_HEAD_DIM * hd, LRU_HEAD_DIM * (hd + 1)
        ri = _dot(conv_bf[:, lo:hi], wgate_ref[hd])
        r = jax.nn.sigmoid(ri[:, :LRU_HEAD_DIM] + ba_ref[:, lo:hi])
        ig = jax.nn.sigmoid(ri[:, LRU_HEAD_DIM:] + bx_ref[:, lo:hi])
        log_a = decay[:, lo:hi] * r
        a_ref[:, lo:hi] = jnp.exp(log_a)
        th = jnp.tanh(log_a)
        b_ref[:, lo:hi] = jnp.sqrt(-2.0 * th / (1.0 - th)) * (ig * conv[:, lo:hi])

    def lru_step(k, carry):
        h, pc = carry
        r0 = pl.multiple_of(k * SUBLANES, SUBLANES)
        a = a_ref[pl.ds(r0, SUBLANES), :]
        h = a * h + b_ref[pl.ds(r0, SUBLANES), :]
        pc = a * pc
        b_ref[pl.ds(r0, SUBLANES), :] = h
        a_ref[pl.ds(r0, SUBLANES), :] = pc
        return h, pc

    h_end, p_end = lax.fori_loop(
        0, S, lru_step,
        (jnp.zeros((SUBLANES, LRU_WIDTH), F32), jnp.ones((SUBLANES, LRU_WIDTH), F32)),
        unroll=4)
    hh, pp = h_end, p_end
    for d in (1, 2, 4):
        hh, pp = pp * _shift_rows(hh, d, 0.0) + hh, pp * _shift_rows(pp, d, 1.0)
    cin = clru_ref[...]
    car_h = _shift_rows(hh, 1, 0.0) + _shift_rows(pp, 1, 1.0) * cin
    hout = hh + pp * cin
    clru_ref[...] = jnp.broadcast_to(hout[NSEG - 1:NSEG, :], (SUBLANES, LRU_WIDTH))

    car_tile = jnp.concatenate([car_h] * S, axis=0)
    y_l = b_ref[...] + a_ref[...] * car_tile
    lru_g = _dot(hbf_ref[...], win_ref[:, C_LRUG:C_LRUG + LRU_WIDTH])
    z_l = _dot((y_l * jax.nn.silu(lru_g)).astype(BF16), wbl_ref[...])

    gate_s = _dot(hbf_ref[...], win_ref[:, C_GS:C_GS + D_MODEL])
    gate_l = _dot(hbf_ref[...], win_ref[:, C_GL:C_GL + D_MODEL])
    merged = jax.nn.sigmoid(gate_s) * z_s + jax.nn.sigmoid(gate_l) * z_l
    mix = _dot(merged.astype(BF16), wout_ref[...])
    inv2 = lax.rsqrt(jnp.mean(mix * mix, axis=-1, keepdims=True) + NORM_EPS)
    for j in range(NSLAB):
        lo, hi = LANES * j, LANES * (j + 1)
        xn_ref[j] = xs_ref[j] + mix[:, lo:hi] * inv2 * gpost_ref[:, lo:hi]

    for nb in range(TM // SUBLANES):
        seg, m = divmod(nb, blocks_per_seg)
        for j in range(NSLAB):
            o_ref[0, SUBLANES * nb:SUBLANES * (nb + 1), LANES * j:LANES * (j + 1)] = (
                xn_ref[j, pl.ds(SUBLANES * NSEG * m + seg, SUBLANES, stride=NSEG), :])
    xnat = o_ref[0]
    emb = _dot(p_ref[0].astype(BF16), wple_ref[...])
    gate = jax.nn.sigmoid(_dot(xnat.astype(BF16), wpleg_ref[...]))
    o_ref[0] = xnat + emb * gate


def _s5_tables(a_re, a_im, log_dt, b_re, b_im, c_re, c_im):
    G, N, C = S5_GROUPS, S5_STATE, S5_GROUP
    lam = lax.complex(a_re, a_im)
    dt = jnp.exp(log_dt)[:, None]
    ldt = lam * dt
    a_bar = jnp.exp(ldt)
    b_bar = ((a_bar - 1.0) / lam)[..., None] * lax.complex(b_re, b_im)

    def powers(e):
        z = jnp.exp(ldt[None] * e[:, None, None].astype(F32))
        z = z.reshape(e.shape[0], NSTATE)
        return jnp.concatenate([jnp.real(z), jnp.imag(z)], axis=1)

    avec = jnp.broadcast_to(powers(jnp.array([1])), (SUBLANES, 2 * NSTATE))
    tpow = powers(jnp.arange(1, S + 1))
    tseg = powers(S * jnp.arange(NSEG))
    tstep = powers(S * jnp.array([1, 2, 4, 8, 0, 0, 0, 0]))

    eye = jnp.eye(G, dtype=F32)
    b_re_full = jnp.einsum('gnc,gh->gchn', jnp.real(b_bar), eye).reshape(S5_WIDTH, NSTATE)
    b_im_full = jnp.einsum('gnc,gh->gchn', jnp.imag(b_bar), eye).reshape(S5_WIDTH, NSTATE)
    c_re_full = jnp.einsum('gcn,gh->hngc', c_re, eye).reshape(NSTATE, S5_WIDTH)
    c_im_full = -jnp.einsum('gcn,gh->hngc', c_im, eye).reshape(NSTATE, S5_WIDTH)
    bpad, cpad = [], []
    for j in range(STATE_TILES):
        lt = j // 4
        rows = slice(LANES * lt, LANES * (lt + 1))
        cols = slice(LANES * j, LANES * (j + 1))
        bpad.append(jnp.concatenate([b_re_full[rows, cols], b_im_full[rows, cols]], axis=1))
        cpad.append(jnp.concatenate([c_re_full[cols, rows], c_im_full[cols, rows]], axis=0))
    return (jnp.stack(bpad).astype(BF16), jnp.stack(cpad).astype(BF16), avec, tpow, tseg, tstep)


def _const_spec(shape):
    nd = len(shape)
    return pl.BlockSpec(shape, lambda b, i, _nd=nd: (0,) * _nd, pipeline_mode=pl.Buffered(1))


def _layer(x, p, g_pre, w_in, s5_a_re, s5_a_im, s5_log_dt, s5_b_re, s5_b_im, s5_c_re, s5_c_im,
           s5_d, w_glu, w_bs, conv_w, conv_b, lru_w_a, lru_b_a, lru_w_x, lru_b_x, lru_lambda,
           w_bl, w_out, g_post, w_ple, w_ple_gate):
    B, L, D = x.shape
    assert D == D_MODEL and L % TM == 0 and S % SUBLANES == 0 and S >= CONV_WIDTH - 1
    bpad, cpad, avec, tpow, tseg, tstep = _s5_tables(
        s5_a_re, s5_a_im, s5_log_dt, s5_b_re, s5_b_im, s5_c_re, s5_c_im)
    wgate = jnp.concatenate([lru_w_a, lru_w_x], axis=-1).astype(BF16)
    row = lambda v: v.reshape(1, -1).astype(F32)
    consts = [
        row(g_pre), w_in.astype(BF16), bpad, cpad, avec, tpow, tseg, tstep, row(s5_d),
        w_glu.astype(BF16), w_bs.astype(BF16), conv_w.astype(F32), row(conv_b), wgate,
        row(lru_b_a), row(lru_b_x), row(lru_lambda), w_bl.astype(BF16), w_out.astype(BF16),
        row(g_post), w_ple.astype(BF16), w_ple_gate.astype(BF16),
    ]
    in_specs = [
        pl.BlockSpec((1, TM, D_MODEL), lambda b, i: (b, i, 0)),
        pl.BlockSpec((1, TM, PLE_DIM), lambda b, i: (b, i, 0)),
    ] + [_const_spec(c.shape) for c in consts]
    scratch = [
        pltpu.VMEM((NSLAB, TM, LANES), F32),
        pltpu.VMEM((TM, D_MODEL), BF16),
        pltpu.VMEM((TM, 2 * NSTATE), F32),
        pltpu.VMEM((TM, 2 * NSTATE), BF16),
        pltpu.VMEM((HALO + TM, LRU_WIDTH), F32),
        pltpu.VMEM((TM, LRU_WIDTH), F32),
        pltpu.VMEM((TM, LRU_WIDTH), F32),
        pltpu.VMEM((NSLAB, TM, LANES), F32),
        pltpu.VMEM((SUBLANES, 2 * NSTATE), F32),
        pltpu.VMEM((SUBLANES, LRU_WIDTH), F32),
        pltpu.VMEM((HALO, LRU_WIDTH), F32),
    ]
    return pl.pallas_call(
        _layer_kernel,
        out_shape=jax.ShapeDtypeStruct((B, L, D_MODEL), x.dtype),
        grid=(B, L // TM),
        in_specs=in_specs,
        out_specs=pl.BlockSpec((1, TM, D_MODEL), lambda b, i: (b, i, 0)),
        scratch_shapes=scratch,
        compiler_params=pltpu.CompilerParams(
            dimension_semantics=("arbitrary", "arbitrary"),
            vmem_limit_bytes=VMEM_LIMIT_BYTES),
        name="hybrid_layer",
    )(x, p, *consts)


def kernel(x, p, g_pre, w_in, s5_a_re, s5_a_im, s5_log_dt, s5_b_re, s5_b_im, s5_c_re, s5_c_im, s5_d, w_glu, w_bs, conv_w, conv_b, lru_w_a, lru_b_a, lru_w_x, lru_b_x, lru_lambda, w_bl, w_out, g_post, w_ple, w_ple_gate):
    params = (g_pre, w_in, s5_a_re, s5_a_im, s5_log_dt, s5_b_re, s5_b_im, s5_c_re, s5_c_im,
              s5_d, w_glu, w_bs, conv_w, conv_b, lru_w_a, lru_b_a, lru_w_x, lru_b_x, lru_lambda,
              w_bl, w_out, g_post, w_ple, w_ple_gate)
    for layer in range(g_pre.shape[0]):
        x = _layer(x, p[layer], *(w[layer] for w in params))
    return x
```

```python
import functools

import jax
import jax.numpy as jnp
from jax import lax
from jax.experimental import pallas as pl
from jax.experimental.pallas import tpu as pltpu

D_MODEL = 1024
PLE_DIM = 256
NORM_EPS = 1e-6
S5_WIDTH = 512
S5_GROUP = 16
S5_GROUPS = S5_WIDTH // S5_GROUP
S5_STATE = 64
NSTATE = S5_GROUPS * S5_STATE
LRU_WIDTH = 1280
LRU_HEADS = 10
LRU_HEAD_DIM = LRU_WIDTH // LRU_HEADS
LRU_C = 8.0
CONV_WIDTH = 4
IN_COLS = 2 * S5_WIDTH + 2 * LRU_WIDTH + 2 * D_MODEL

C_S5X = 0
C_S5G = S5_WIDTH
C_LRUX = 2 * S5_WIDTH
C_LRUG = C_LRUX + LRU_WIDTH
C_GS = C_LRUG + LRU_WIDTH
C_GL = C_GS + D_MODEL

SUBLANES = 8
LANES = 128
NSEG = SUBLANES
TM = 256
S = TM // NSEG
NSLAB = D_MODEL // LANES
STATE_TILES = NSTATE // LANES
HALO = (CONV_WIDTH - 1) * SUBLANES
VMEM_LIMIT_BYTES = 60 * 1024 * 1024

F32 = jnp.float32
BF16 = jnp.bfloat16


def _dot(a, b):
    return jnp.dot(a, b, preferred_element_type=F32)


def _shift_rows(x, d, fill):
    rolled = pltpu.roll(x, d, axis=0)
    row = lax.broadcasted_iota(jnp.int32, x.shape, 0)
    return jnp.where(row >= d, rolled, fill)


def _cmul(ar, ai, xr, xi):
    return ar * xr - ai * xi, ar * xi + ai * xr


def _permuted_rows(nb):
    seg, m = divmod(nb, S // SUBLANES)
    return pl.ds(SUBLANES * NSEG * m + seg, SUBLANES, stride=NSEG)


def _layer_kernel(tiles_per_seq,
                  x_ref, p_ref, gpre_ref, win_ref, bpad_ref, cpad_ref, avec_ref,
                  tseg_ref, tstep_ref, s5d_ref, wglu_ref, wbs_ref,
                  convw_ref, convb_ref, wgate_ref, ba_ref, bx_ref, lam_ref,
                  wbl_ref, wout_ref, gpost_ref, wple_ref, wpleg_ref,
                  o_ref,
                  xs_ref, hbf_ref, bu_ref, sbf_ref, ext_ref, a_ref, b_ref,
                  xn_ref, merged_ref, xprev_ref, cs5_ref, clru_ref, hist_ref):
    s = pl.program_id(0)

    @pl.when(s == 0)
    def _():
        merged_ref[...] = jnp.zeros_like(merged_ref)
        xprev_ref[...] = jnp.zeros_like(xprev_ref)

    @pl.when(lax.rem(s, tiles_per_seq) == 0)
    def _():
        cs5_ref[...] = jnp.zeros_like(cs5_ref)
        clru_ref[...] = jnp.zeros_like(clru_ref)
        hist_ref[...] = jnp.zeros_like(hist_ref)

    mix = _dot(merged_ref[...], wout_ref[...])
    inv2 = lax.rsqrt(jnp.mean(mix * mix, axis=-1, keepdims=True) + NORM_EPS)
    for j in range(NSLAB):
        lo, hi = LANES * j, LANES * (j + 1)
        xn_ref[j] = mix[:, lo:hi] * inv2 * gpost_ref[:, lo:hi]
    for nb in range(TM // SUBLANES):
        rows = slice(SUBLANES * nb, SUBLANES * (nb + 1))
        for j in range(NSLAB):
            lanes = slice(LANES * j, LANES * (j + 1))
            o_ref[0, rows, lanes] = xprev_ref[rows, lanes] + xn_ref[j, _permuted_rows(nb), :]
    xnat = o_ref[0]
    emb = _dot(p_ref[0].astype(BF16), wple_ref[...])
    gate = jnp.tanh(_dot(xnat.astype(BF16), wpleg_ref[...])) + 1.0
    o_ref[0] = xnat + emb * gate

    for nb in range(TM // SUBLANES):
        for j in range(NSLAB):
            xs_ref[j, _permuted_rows(nb), :] = (
                x_ref[0, SUBLANES * nb:SUBLANES * (nb + 1), LANES * j:LANES * (j + 1)])

    ss = jnp.zeros((TM, LANES), F32)
    for j in range(NSLAB):
        v = xs_ref[j]
        ss = ss + v * v
    inv = lax.rsqrt(jnp.sum(ss, axis=-1, keepdims=True) * (1.0 / D_MODEL) + NORM_EPS)
    for j in range(NSLAB):
        hbf_ref[:, LANES * j:LANES * (j + 1)] = (
            xs_ref[j] * inv * gpre_ref[:, LANES * j:LANES * (j + 1)]).astype(BF16)

    u = _dot(hbf_ref[...], win_ref[:, C_S5X:C_S5X + S5_WIDTH])
    u_bf = u.astype(BF16)
    for j in range(STATE_TILES):
        lt = j // 4
        bu_j = _dot(u_bf[:, LANES * lt:LANES * (lt + 1)], bpad_ref[j])
        bu_ref[:, LANES * j:LANES * (j + 1)] = bu_j[:, :LANES]
        bu_ref[:, NSTATE + LANES * j:NSTATE + LANES * (j + 1)] = bu_j[:, LANES:]

    ar = avec_ref[:, :NSTATE]
    ai = avec_ref[:, NSTATE:]
    sr = bu_ref[0:SUBLANES, :NSTATE]
    si = bu_ref[0:SUBLANES, NSTATE:]
    for k in range(1, S):
        r0 = SUBLANES * k
        nr = ar * sr - ai * si + bu_ref[r0:r0 + SUBLANES, :NSTATE]
        ni = ar * si + ai * sr + bu_ref[r0:r0 + SUBLANES, NSTATE:]
        bu_ref[r0:r0 + SUBLANES, :NSTATE] = nr
        bu_ref[r0:r0 + SUBLANES, NSTATE:] = ni
        sr, si = nr, ni

    ir, ii = sr, si
    for step, d in enumerate((1, 2, 4)):
        mr, mi = _cmul(tstep_ref[step, :, :NSTATE], tstep_ref[step, :, NSTATE:],
                       _shift_rows(ir, d, 0.0), _shift_rows(ii, d, 0.0))
        ir, ii = ir + mr, ii + mi
    cin_r = cs5_ref[:, :NSTATE]
    cin_i = cs5_ref[:, NSTATE:]
    qr, qi = _cmul(tseg_ref[:, :NSTATE], tseg_ref[:, NSTATE:], cin_r, cin_i)
    car_r = _shift_rows(ir, 1, 0.0) + qr
    car_i = _shift_rows(ii, 1, 0.0) + qi
    wr, wi = _cmul(tstep_ref[3, :, :NSTATE], tstep_ref[3, :, NSTATE:], cin_r, cin_i)
    out_r = ir + wr
    out_i = ii + wi
    cs5_ref[:, :NSTATE] = jnp.broadcast_to(out_r[NSEG - 1:NSEG, :], (SUBLANES, NSTATE))
    cs5_ref[:, NSTATE:] = jnp.broadcast_to(out_i[NSEG - 1:NSEG, :], (SUBLANES, NSTATE))

    fr, fi = _cmul(ar, ai, car_r, car_i)
    for kk in range(S // 2):
        r0 = 2 * SUBLANES * kk
        gr, gi = _cmul(ar, ai, fr, fi)
        tr = jnp.concatenate([bu_ref[r0:r0 + SUBLANES, :NSTATE] + fr,
                              bu_ref[r0 + SUBLANES:r0 + 2 * SUBLANES, :NSTATE] + gr], axis=0)
        ti = jnp.concatenate([bu_ref[r0:r0 + SUBLANES, NSTATE:] + fi,
                              bu_ref[r0 + SUBLANES:r0 + 2 * SUBLANES, NSTATE:] + gi], axis=0)
        sbf_ref[r0:r0 + 2 * SUBLANES, :NSTATE] = tr.astype(BF16)
        sbf_ref[r0:r0 + 2 * SUBLANES, NSTATE:] = ti.astype(BF16)
        if kk + 1 < S // 2:
            fr, fi = _cmul(ar, ai, gr, gi)

    y_tiles = []
    for lt in range(S5_WIDTH // LANES):
        acc = None
        for j in range(4 * lt, 4 * lt + 4):
            lhs = jnp.concatenate([sbf_ref[:, LANES * j:LANES * (j + 1)],
                                   sbf_ref[:, NSTATE + LANES * j:NSTATE + LANES * (j + 1)]], axis=1)
            t = _dot(lhs, cpad_ref[j])
            acc = t if acc is None else acc + t
        y_tiles.append(acc)
    y_s = jnp.concatenate(y_tiles, axis=1) + s5d_ref[...] * u
    glu = _dot(jax.nn.gelu(y_s).astype(BF16), wglu_ref[...])
    s5_g = _dot(hbf_ref[...], win_ref[:, C_S5G:C_S5G + S5_WIDTH])
    y2 = (glu[:, :S5_WIDTH] * (jnp.tanh(glu[:, S5_WIDTH:]) + 1.0)) * (s5_g * (jnp.tanh(s5_g) + 1.0))
    z_s = _dot(y2.astype(BF16), wbs_ref[...])

    ext_ref[HALO:, :] = _dot(hbf_ref[...], win_ref[:, C_LRUX:C_LRUX + LRU_WIDTH])
    row8 = lax.broadcasted_iota(jnp.int32, (SUBLANES, LRU_WIDTH), 0)
    for d in range(CONV_WIDTH - 1):
        cur = ext_ref[TM + SUBLANES * d:TM + SUBLANES * (d + 1), :]
        old = hist_ref[SUBLANES * d:SUBLANES * (d + 1), :]
        ext_ref[SUBLANES * d:SUBLANES * (d + 1), :] = jnp.where(
            row8 == 0, pltpu.roll(old, 1, axis=0), pltpu.roll(cur, 1, axis=0))
        hist_ref[SUBLANES * d:SUBLANES * (d + 1), :] = cur
    conv = convb_ref[...] + convw_ref[CONV_WIDTH - 1:CONV_WIDTH, :] * ext_ref[HALO:HALO + TM, :]
    for t in range(1, CONV_WIDTH):
        conv = conv + (convw_ref[CONV_WIDTH - 1 - t:CONV_WIDTH - t, :]
                       * ext_ref[HALO - SUBLANES * t:HALO - SUBLANES * t + TM, :])
    conv_bf = conv.astype(BF16)
    half_decay = (-0.5 * LRU_C) * jax.nn.softplus(-lam_ref[...])
    for hd in range(LRU_HEADS):
        lo, hi = LRU_HEAD_DIM * hd, LRU_HEAD_DIM * (hd + 1)
        ri = _dot(conv_bf[:, lo:hi], wgate_ref[hd])
        th_r = jnp.tanh(ri[:, :LRU_HEAD_DIM] + ba_ref[:, lo:hi])
        th_i = jnp.tanh(ri[:, LRU_HEAD_DIM:] + bx_ref[:, lo:hi])
        log_a = half_decay[:, lo:hi] * (th_r + 1.0)
        a_ref[:, lo:hi] = jnp.exp(log_a)
        th = jnp.tanh(log_a)
        nd = (-2.0 * th) * (1.0 - th)
        half_mult = jnp.where(nd > 0.0, -th * lax.rsqrt(nd), 0.0)
        b_ref[:, lo:hi] = half_mult * (th_i + 1.0) * conv[:, lo:hi]

    h_end = b_ref[0:SUBLANES, :]
    p_end = a_ref[0:SUBLANES, :]
    for k in range(1, S):
        r0 = SUBLANES * k
        a = a_ref[r0:r0 + SUBLANES, :]
        h_end = a * h_end + b_ref[r0:r0 + SUBLANES, :]
        p_end = a * p_end
        b_ref[r0:r0 + SUBLANES, :] = h_end
        a_ref[r0:r0 + SUBLANES, :] = p_end
    hh, pp = h_end, p_end
    for d in (1, 2, 4):
        hh, pp = pp * _shift_rows(hh, d, 0.0) + hh, pp * _shift_rows(pp, d, 1.0)
    cin = clru_ref[...]
    car_h = _shift_rows(hh, 1, 0.0) + _shift_rows(pp, 1, 1.0) * cin
    hout = hh + pp * cin
    clru_ref[...] = jnp.broadcast_to(hout[NSEG - 1:NSEG, :], (SUBLANES, LRU_WIDTH))

    car_tile = jnp.concatenate([car_h] * S, axis=0)
    y_l = b_ref[...] + a_ref[...] * car_tile
    lru_g = _dot(hbf_ref[...], win_ref[:, C_LRUG:C_LRUG + LRU_WIDTH])
    z_l = _dot((y_l * (lru_g * (jnp.tanh(lru_g) + 1.0))).astype(BF16), wbl_ref[...])

    gate_s = _dot(hbf_ref[...], win_ref[:, C_GS:C_GS + D_MODEL])
    gate_l = _dot(hbf_ref[...], win_ref[:, C_GL:C_GL + D_MODEL])
    merged = (jnp.tanh(gate_s) + 1.0) * z_s + (jnp.tanh(gate_l) + 1.0) * z_l
    merged_ref[...] = merged.astype(BF16)
    xprev_ref[...] = x_ref[0]


def _s5_tables(a_re, a_im, log_dt, b_re, b_im, c_re, c_im):
    depth = a_re.shape[0]
    dt = jnp.exp(log_dt)[..., None]
    pr, pi = a_re * dt, a_im * dt
    mag = jnp.exp(pr)
    abr, abi = mag * jnp.cos(pi), mag * jnp.sin(pi)
    den = a_re * a_re + a_im * a_im
    zr = ((abr - 1.0) * a_re + abi * a_im) / den
    zi = (abi * a_re - (abr - 1.0) * a_im) / den
    bbr = zr[..., None] * b_re - zi[..., None] * b_im
    bbi = zr[..., None] * b_im + zi[..., None] * b_re

    def powers(e):
        e = jnp.asarray(e, F32)[None, :, None, None]
        m = jnp.exp(pr[:, None] * e)
        ang = pi[:, None] * e
        k = e.shape[1]
        return jnp.concatenate([(m * jnp.cos(ang)).reshape(depth, k, NSTATE),
                                (m * jnp.sin(ang)).reshape(depth, k, NSTATE)], axis=-1)

    avec = jnp.broadcast_to(powers([1]), (depth, SUBLANES, 2 * NSTATE))
    tseg = powers([S * q for q in range(NSEG)])
    tstep = jnp.broadcast_to(powers([S, 2 * S, 4 * S, 8 * S])[:, :, None, :],
                             (depth, 4, SUBLANES, 2 * NSTATE))

    eye = jnp.eye(S5_GROUPS, dtype=F32)
    b_re_full = jnp.einsum('lgnc,gh->lgchn', bbr, eye).reshape(depth, S5_WIDTH, NSTATE)
    b_im_full = jnp.einsum('lgnc,gh->lgchn', bbi, eye).reshape(depth, S5_WIDTH, NSTATE)
    c_re_full = jnp.einsum('lgcn,gh->lhngc', c_re, eye).reshape(depth, NSTATE, S5_WIDTH)
    c_im_full = -jnp.einsum('lgcn,gh->lhngc', c_im, eye).reshape(depth, NSTATE, S5_WIDTH)
    bpad, cpad = [], []
    for j in range(STATE_TILES):
        lt = j // 4
        rows = slice(LANES * lt, LANES * (lt + 1))
        cols = slice(LANES * j, LANES * (j + 1))
        bpad.append(jnp.concatenate([b_re_full[:, rows, cols], b_im_full[:, rows, cols]], axis=2))
        cpad.append(jnp.concatenate([c_re_full[:, cols, rows], c_im_full[:, cols, rows]], axis=1))
    return (jnp.stack(bpad, axis=1).astype(BF16), jnp.stack(cpad, axis=1).astype(BF16),
            avec, tseg, tstep)


def _layer_spec(shape, layer):
    nd = len(shape) - 1
    return pl.BlockSpec((None,) + tuple(shape[1:]), lambda s: (layer,) + (0,) * nd,
                        pipeline_mode=pl.Buffered(1))


def kernel(x, p, g_pre, w_in, s5_a_re, s5_a_im, s5_log_dt, s5_b_re, s5_b_im, s5_c_re, s5_c_im, s5_d, w_glu, w_bs, conv_w, conv_b, lru_w_a, lru_b_a, lru_w_x, lru_b_x, lru_lambda, w_bl, w_out, g_post, w_ple, w_ple_gate):
    B, L, D = x.shape
    depth = g_pre.shape[0]
    assert D == D_MODEL and L % TM == 0 and S % SUBLANES == 0 and S >= CONV_WIDTH - 1
    nt = L // TM
    n_tiles = B * nt
    bpad, cpad, avec, tseg, tstep = _s5_tables(
        s5_a_re, s5_a_im, s5_log_dt, s5_b_re, s5_b_im, s5_c_re, s5_c_im)
    col_scale = jnp.ones((IN_COLS,), F32).at[C_S5G:C_S5G + S5_WIDTH].set(0.5).at[C_LRUG:].set(0.5)
    w_in_k = (w_in * col_scale).astype(BF16)
    w_glu_k = (0.5 * w_glu).astype(BF16)
    wgate = (0.5 * jnp.concatenate([lru_w_a, lru_w_x], axis=-1)).astype(BF16)
    row = lambda v: v.reshape(depth, 1, -1).astype(F32)
    consts = [
        row(g_pre), w_in_k, bpad, cpad, avec, tseg, tstep, row(s5_d),
        w_glu_k, (0.5 * w_bs).astype(BF16), conv_w.astype(F32), row(conv_b), wgate,
        row(0.5 * lru_b_a), row(0.5 * lru_b_x), row(lru_lambda), (0.5 * w_bl).astype(BF16),
        w_out.astype(BF16), row(g_post), (0.5 * w_ple).astype(BF16), (0.5 * w_ple_gate).astype(BF16),
    ]
    scratch = [
        pltpu.VMEM((NSLAB, TM, LANES), F32),
        pltpu.VMEM((TM, D_MODEL), BF16),
        pltpu.VMEM((TM, 2 * NSTATE), F32),
        pltpu.VMEM((TM, 2 * NSTATE), BF16),
        pltpu.VMEM((HALO + TM, LRU_WIDTH), F32),
        pltpu.VMEM((TM, LRU_WIDTH), F32),
        pltpu.VMEM((TM, LRU_WIDTH), F32),
        pltpu.VMEM((NSLAB, TM, LANES), F32),
        pltpu.VMEM((TM, D_MODEL), BF16),
        pltpu.VMEM((TM, D_MODEL), F32),
        pltpu.VMEM((SUBLANES, 2 * NSTATE), F32),
        pltpu.VMEM((SUBLANES, LRU_WIDTH), F32),
        pltpu.VMEM((HALO, LRU_WIDTH), F32),
    ]

    def main_tile(s):
        t = jnp.minimum(s, n_tiles - 1)
        return t // nt, t % nt

    def tail_tile(s):
        t = jnp.maximum(s - 1, 0)
        return t // nt, t % nt

    for layer in range(depth):
        in_specs = [
            pl.BlockSpec((1, TM, D_MODEL), lambda s: main_tile(s) + (0,)),
            pl.BlockSpec((None, 1, TM, PLE_DIM), lambda s, layer=layer: (layer,) + tail_tile(s) + (0,)),
        ] + [_layer_spec(c.shape, layer) for c in consts]
        x = pl.pallas_call(
            functools.partial(_layer_kernel, nt),
            out_shape=jax.ShapeDtypeStruct((B, L, D_MODEL), x.dtype),
            grid=(n_tiles + 1,),
            in_specs=in_specs,
            out_specs=pl.BlockSpec((1, TM, D_MODEL), lambda s: tail_tile(s) + (0,)),
            scratch_shapes=scratch,
            compiler_params=pltpu.CompilerParams(
                dimension_semantics=("arbitrary",),
                vmem_limit_bytes=VMEM_LIMIT_BYTES),
            name="hybrid_layer",
        )(x, p, *consts)
    return x
```

```python
import functools

import jax
import jax.numpy as jnp
from jax import lax
from jax.experimental import pallas as pl
from jax.experimental.pallas import tpu as pltpu

D_MODEL = 1024
PLE_DIM = 256
NORM_EPS = 1e-6
S5_WIDTH = 512
S5_GROUP = 16
S5_GROUPS = S5_WIDTH // S5_GROUP
S5_STATE = 64
NSTATE = S5_GROUPS * S5_STATE
LRU_WIDTH = 1280
LRU_HEADS = 10
LRU_HEAD_DIM = LRU_WIDTH // LRU_HEADS
LRU_C = 8.0
CONV_WIDTH = 4
IN_COLS = 2 * S5_WIDTH + 2 * LRU_WIDTH + 2 * D_MODEL

C_S5X = 0
C_S5G = S5_WIDTH
C_LRUX = 2 * S5_WIDTH
C_LRUG = C_LRUX + LRU_WIDTH
C_GS = C_LRUG + LRU_WIDTH
C_GL = C_GS + D_MODEL

SUBLANES = 8
LANES = 128
NSEG = SUBLANES
TM = 256
S = TM // NSEG
NSLAB = D_MODEL // LANES
STATE_TILES = NSTATE // LANES
HALO = (CONV_WIDTH - 1) * SUBLANES
VMEM_LIMIT_BYTES = 60 * 1024 * 1024

F32 = jnp.float32
BF16 = jnp.bfloat16


def _dot(a, b):
    return jnp.dot(a, b, preferred_element_type=F32)


def _shift_rows(x, d, fill):
    rolled = pltpu.roll(x, d, axis=0)
    row = lax.broadcasted_iota(jnp.int32, x.shape, 0)
    return jnp.where(row >= d, rolled, fill)


def _cmul(ar, ai, xr, xi):
    return ar * xr - ai * xi, ar * xi + ai * xr


def _permuted_rows(nb):
    seg, m = divmod(nb, S // SUBLANES)
    return pl.ds(SUBLANES * NSEG * m + seg, SUBLANES, stride=NSEG)


def _layer_kernel(tiles_per_seq,
                  x_ref, p_ref, gpre_ref, win_ref, bpad_ref, cpad_ref, avec_ref,
                  tseg_ref, tstep_ref, s5d_ref, wglu_ref, wbs_ref,
                  convw_ref, convb_ref, wgate_ref, ba_ref, bx_ref, lam_ref,
                  wbl_ref, wout_ref, gpost_ref, wple_ref, wpleg_ref,
                  o_ref,
                  xs_ref, hbf_ref, bu_ref, sbf_ref, ext_ref, a_ref, b_ref,
                  xn_ref, merged_ref, xprev_ref, gs_ref, gl_ref, lg_ref,
                  cs5_ref, clru_ref, hist_ref):
    s = pl.program_id(0)

    @pl.when(s == 0)
    def _():
        merged_ref[...] = jnp.zeros_like(merged_ref)
        xprev_ref[...] = jnp.zeros_like(xprev_ref)

    @pl.when(lax.rem(s, tiles_per_seq) == 0)
    def _():
        cs5_ref[...] = jnp.zeros_like(cs5_ref)
        clru_ref[...] = jnp.zeros_like(clru_ref)
        hist_ref[...] = jnp.zeros_like(hist_ref)

    mix = _dot(merged_ref[...], wout_ref[...])

    for nb in range(TM // SUBLANES):
        for j in range(NSLAB):
            xs_ref[j, _permuted_rows(nb), :] = (
                x_ref[0, SUBLANES * nb:SUBLANES * (nb + 1), LANES * j:LANES * (j + 1)])

    ss = jnp.zeros((TM, LANES), F32)
    for j in range(NSLAB):
        v = xs_ref[j]
        ss = ss + v * v
    inv = lax.rsqrt(jnp.sum(ss, axis=-1, keepdims=True) * (1.0 / D_MODEL) + NORM_EPS)
    for j in range(NSLAB):
        hbf_ref[:, LANES * j:LANES * (j + 1)] = (
            xs_ref[j] * inv * gpre_ref[:, LANES * j:LANES * (j + 1)]).astype(BF16)

    u = _dot(hbf_ref[...], win_ref[:, C_S5X:C_S5X + S5_WIDTH])
    u_bf = u.astype(BF16)
    for j in range(STATE_TILES):
        lt = j // 4
        bu_j = _dot(u_bf[:, LANES * lt:LANES * (lt + 1)], bpad_ref[j])
        bu_ref[:, LANES * j:LANES * (j + 1)] = bu_j[:, :LANES]
        bu_ref[:, NSTATE + LANES * j:NSTATE + LANES * (j + 1)] = bu_j[:, LANES:]

    inv2 = lax.rsqrt(jnp.mean(mix * mix, axis=-1, keepdims=True) + NORM_EPS)
    for j in range(NSLAB):
        lo, hi = LANES * j, LANES * (j + 1)
        xn_ref[j] = mix[:, lo:hi] * inv2 * gpost_ref[:, lo:hi]
    for nb in range(TM // SUBLANES):
        rows = slice(SUBLANES * nb, SUBLANES * (nb + 1))
        for j in range(NSLAB):
            lanes = slice(LANES * j, LANES * (j + 1))
            o_ref[0, rows, lanes] = xprev_ref[rows, lanes] + xn_ref[j, _permuted_rows(nb), :]

    gs_ref[...] = _dot(hbf_ref[...], win_ref[:, C_GS:C_GS + D_MODEL])
    ext_ref[HALO:, :] = _dot(hbf_ref[...], win_ref[:, C_LRUX:C_LRUX + LRU_WIDTH])

    ar = avec_ref[:, :NSTATE]
    ai = avec_ref[:, NSTATE:]
    sr = bu_ref[0:SUBLANES, :NSTATE]
    si = bu_ref[0:SUBLANES, NSTATE:]
    for k in range(1, S):
        r0 = SUBLANES * k
        nr = ar * sr - ai * si + bu_ref[r0:r0 + SUBLANES, :NSTATE]
        ni = ar * si + ai * sr + bu_ref[r0:r0 + SUBLANES, NSTATE:]
        bu_ref[r0:r0 + SUBLANES, :NSTATE] = nr
        bu_ref[r0:r0 + SUBLANES, NSTATE:] = ni
        sr, si = nr, ni

    xnat = o_ref[0]
    emb = _dot(p_ref[0].astype(BF16), wple_ref[...])
    gate = jnp.tanh(_dot(xnat.astype(BF16), wpleg_ref[...])) + 1.0
    o_ref[0] = xnat + emb * gate

    ir, ii = sr, si
    for step, d in enumerate((1, 2, 4)):
        mr, mi = _cmul(tstep_ref[step, :, :NSTATE], tstep_ref[step, :, NSTATE:],
                       _shift_rows(ir, d, 0.0), _shift_rows(ii, d, 0.0))
        ir, ii = ir + mr, ii + mi
    cin_r = cs5_ref[:, :NSTATE]
    cin_i = cs5_ref[:, NSTATE:]
    qr, qi = _cmul(tseg_ref[:, :NSTATE], tseg_ref[:, NSTATE:], cin_r, cin_i)
    car_r = _shift_rows(ir, 1, 0.0) + qr
    car_i = _shift_rows(ii, 1, 0.0) + qi
    wr, wi = _cmul(tstep_ref[3, :, :NSTATE], tstep_ref[3, :, NSTATE:], cin_r, cin_i)
    out_r = ir + wr
    out_i = ii + wi
    cs5_ref[:, :NSTATE] = jnp.broadcast_to(out_r[NSEG - 1:NSEG, :], (SUBLANES, NSTATE))
    cs5_ref[:, NSTATE:] = jnp.broadcast_to(out_i[NSEG - 1:NSEG, :], (SUBLANES, NSTATE))

    row8 = lax.broadcasted_iota(jnp.int32, (SUBLANES, LRU_WIDTH), 0)
    for d in range(CONV_WIDTH - 1):
        cur = ext_ref[TM + SUBLANES * d:TM + SUBLANES * (d + 1), :]
        old = hist_ref[SUBLANES * d:SUBLANES * (d + 1), :]
        ext_ref[SUBLANES * d:SUBLANES * (d + 1), :] = jnp.where(
            row8 == 0, pltpu.roll(old, 1, axis=0), pltpu.roll(cur, 1, axis=0))
        hist_ref[SUBLANES * d:SUBLANES * (d + 1), :] = cur
    conv = convb_ref[...] + convw_ref[CONV_WIDTH - 1:CONV_WIDTH, :] * ext_ref[HALO:HALO + TM, :]
    for t in range(1, CONV_WIDTH):
        conv = conv + (convw_ref[CONV_WIDTH - 1 - t:CONV_WIDTH - t, :]
                       * ext_ref[HALO - SUBLANES * t:HALO - SUBLANES * t + TM, :])
    conv_bf = conv.astype(BF16)
    half_decay = (-0.5 * LRU_C) * jax.nn.softplus(-lam_ref[...])

    gl_ref[...] = _dot(hbf_ref[...], win_ref[:, C_GL:C_GL + D_MODEL])

    fr, fi = _cmul(ar, ai, car_r, car_i)
    for kk in range(S // 2):
        r0 = 2 * SUBLANES * kk
        gr, gi = _cmul(ar, ai, fr, fi)
        tr = jnp.concatenate([bu_ref[r0:r0 + SUBLANES, :NSTATE] + fr,
                              bu_ref[r0 + SUBLANES:r0 + 2 * SUBLANES, :NSTATE] + gr], axis=0)
        ti = jnp.concatenate([bu_ref[r0:r0 + SUBLANES, NSTATE:] + fi,
                              bu_ref[r0 + SUBLANES:r0 + 2 * SUBLANES, NSTATE:] + gi], axis=0)
        sbf_ref[r0:r0 + 2 * SUBLANES, :NSTATE] = tr.astype(BF16)
        sbf_ref[r0:r0 + 2 * SUBLANES, NSTATE:] = ti.astype(BF16)
        if kk + 1 < S // 2:
            fr, fi = _cmul(ar, ai, gr, gi)

    y_tiles = []

    def c_group(lt):
        acc = None
        for j in range(4 * lt, 4 * lt + 4):
            lhs = jnp.concatenate([sbf_ref[:, LANES * j:LANES * (j + 1)],
                                   sbf_ref[:, NSTATE + LANES * j:NSTATE + LANES * (j + 1)]], axis=1)
            t = _dot(lhs, cpad_ref[j])
            acc = t if acc is None else acc + t
        y_tiles.append(acc)

    def lg_chunk(c):
        lo, hi = 2 * LANES * c, 2 * LANES * (c + 1)
        lg_ref[:, lo:hi] = _dot(hbf_ref[...], win_ref[:, C_LRUG + lo:C_LRUG + hi])

    filler = []
    for q in range(max(S5_WIDTH // LANES, LRU_WIDTH // (2 * LANES))):
        if q < S5_WIDTH // LANES:
            filler.append(functools.partial(c_group, q))
        if q < LRU_WIDTH // (2 * LANES):
            filler.append(functools.partial(lg_chunk, q))

    for hd in range(LRU_HEADS):
        lo, hi = LRU_HEAD_DIM * hd, LRU_HEAD_DIM * (hd + 1)
        ri = _dot(conv_bf[:, lo:hi], wgate_ref[hd])
        th_r = jnp.tanh(ri[:, :LRU_HEAD_DIM] + ba_ref[:, lo:hi])
        th_i = jnp.tanh(ri[:, LRU_HEAD_DIM:] + bx_ref[:, lo:hi])
        log_a = half_decay[:, lo:hi] * (th_r + 1.0)
        a_ref[:, lo:hi] = jnp.exp(log_a)
        th = jnp.tanh(log_a)
        nd = (-2.0 * th) * (1.0 - th)
        half_mult = jnp.where(nd > 0.0, -th * lax.rsqrt(nd), 0.0)
        b_ref[:, lo:hi] = half_mult * (th_i + 1.0) * conv[:, lo:hi]
        if filler:
            filler.pop(0)()
    while filler:
        filler.pop(0)()

    y_s = jnp.concatenate(y_tiles, axis=1) + s5d_ref[...] * u

    h_end = b_ref[0:SUBLANES, :]
    p_end = a_ref[0:SUBLANES, :]
    for k in range(1, S):
        r0 = SUBLANES * k
        a = a_ref[r0:r0 + SUBLANES, :]
        h_end = a * h_end + b_ref[r0:r0 + SUBLANES, :]
        p_end = a * p_end
        b_ref[r0:r0 + SUBLANES, :] = h_end
        a_ref[r0:r0 + SUBLANES, :] = p_end
    hh, pp = h_end, p_end
    for d in (1, 2, 4):
        hh, pp = pp * _shift_rows(hh, d, 0.0) + hh, pp * _shift_rows(pp, d, 1.0)
    cin = clru_ref[...]
    car_h = _shift_rows(hh, 1, 0.0) + _shift_rows(pp, 1, 1.0) * cin
    hout = hh + pp * cin
    clru_ref[...] = jnp.broadcast_to(hout[NSEG - 1:NSEG, :], (SUBLANES, LRU_WIDTH))
    car_tile = jnp.concatenate([car_h] * S, axis=0)
    y_l = b_ref[...] + a_ref[...] * car_tile

    glu = _dot(jax.nn.gelu(y_s).astype(BF16), wglu_ref[...])
    s5_g = _dot(hbf_ref[...], win_ref[:, C_S5G:C_S5G + S5_WIDTH])
    y2 = (glu[:, :S5_WIDTH] * (jnp.tanh(glu[:, S5_WIDTH:]) + 1.0)) * (s5_g * (jnp.tanh(s5_g) + 1.0))
    z_s = _dot(y2.astype(BF16), wbs_ref[...])

    lru_g = lg_ref[...]
    z_l = _dot((y_l * (lru_g * (jnp.tanh(lru_g) + 1.0))).astype(BF16), wbl_ref[...])

    merged = (jnp.tanh(gs_ref[...]) + 1.0) * z_s + (jnp.tanh(gl_ref[...]) + 1.0) * z_l
    merged_ref[...] = merged.astype(BF16)
    xprev_ref[...] = x_ref[0]


def _s5_tables(a_re, a_im, log_dt, b_re, b_im, c_re, c_im):
    depth = a_re.shape[0]
    dt = jnp.exp(log_dt)[..., None]
    pr, pi = a_re * dt, a_im * dt
    mag = jnp.exp(pr)
    abr, abi = mag * jnp.cos(pi), mag * jnp.sin(pi)
    den = a_re * a_re + a_im * a_im
    zr = ((abr - 1.0) * a_re + abi * a_im) / den
    zi = (abi * a_re - (abr - 1.0) * a_im) / den
    bbr = zr[..., None] * b_re - zi[..., None] * b_im
    bbi = zr[..., None] * b_im + zi[..., None] * b_re

    def powers(e):
        e = jnp.asarray(e, F32)[None, :, None, None]
        m = jnp.exp(pr[:, None] * e)
        ang = pi[:, None] * e
        k = e.shape[1]
        return jnp.concatenate([(m * jnp.cos(ang)).reshape(depth, k, NSTATE),
                                (m * jnp.sin(ang)).reshape(depth, k, NSTATE)], axis=-1)

    avec = jnp.broadcast_to(powers([1]), (depth, SUBLANES, 2 * NSTATE))
    tseg = powers([S * q for q in range(NSEG)])
    tstep = jnp.broadcast_to(powers([S, 2 * S, 4 * S, 8 * S])[:, :, None, :],
                             (depth, 4, SUBLANES, 2 * NSTATE))

    eye = jnp.eye(S5_GROUPS, dtype=F32)
    b_re_full = jnp.einsum('lgnc,gh->lgchn', bbr, eye).reshape(depth, S5_WIDTH, NSTATE)
    b_im_full = jnp.einsum('lgnc,gh->lgchn', bbi, eye).reshape(depth, S5_WIDTH, NSTATE)
    c_re_full = jnp.einsum('lgcn,gh->lhngc', c_re, eye).reshape(depth, NSTATE, S5_WIDTH)
    c_im_full = -jnp.einsum('lgcn,gh->lhngc', c_im, eye).reshape(depth, NSTATE, S5_WIDTH)
    bpad, cpad = [], []
    for j in range(STATE_TILES):
        lt = j // 4
        rows = slice(LANES * lt, LANES * (lt + 1))
        cols = slice(LANES * j, LANES * (j + 1))
        bpad.append(jnp.concatenate([b_re_full[:, rows, cols], b_im_full[:, rows, cols]], axis=2))
        cpad.append(jnp.concatenate([c_re_full[:, cols, rows], c_im_full[:, cols, rows]], axis=1))
    return (jnp.stack(bpad, axis=1).astype(BF16), jnp.stack(cpad, axis=1).astype(BF16),
            avec, tseg, tstep)


def _layer_spec(shape, layer):
    nd = len(shape) - 1
    return pl.BlockSpec((None,) + tuple(shape[1:]), lambda s: (layer,) + (0,) * nd,
                        pipeline_mode=pl.Buffered(1))


def kernel(x, p, g_pre, w_in, s5_a_re, s5_a_im, s5_log_dt, s5_b_re, s5_b_im, s5_c_re, s5_c_im, s5_d, w_glu, w_bs, conv_w, conv_b, lru_w_a, lru_b_a, lru_w_x, lru_b_x, lru_lambda, w_bl, w_out, g_post, w_ple, w_ple_gate):
    B, L, D = x.shape
    depth = g_pre.shape[0]
    assert D == D_MODEL and L % TM == 0 and S % SUBLANES == 0 and S >= CONV_WIDTH - 1
    nt = L // TM
    n_tiles = B * nt
    bpad, cpad, avec, tseg, tstep = _s5_tables(
        s5_a_re, s5_a_im, s5_log_dt, s5_b_re, s5_b_im, s5_c_re, s5_c_im)
    col_scale = jnp.ones((IN_COLS,), F32).at[C_S5G:C_S5G + S5_WIDTH].set(0.5).at[C_LRUG:].set(0.5)
    w_in_k = (w_in * col_scale).astype(BF16)
    w_glu_k = (0.5 * w_glu).astype(BF16)
    wgate = (0.5 * jnp.concatenate([lru_w_a, lru_w_x], axis=-1)).astype(BF16)
    row = lambda v: v.reshape(depth, 1, -1).astype(F32)
    consts = [
        row(g_pre), w_in_k, bpad, cpad, avec, tseg, tstep, row(s5_d),
        w_glu_k, (0.5 * w_bs).astype(BF16), conv_w.astype(F32), row(conv_b), wgate,
        row(0.5 * lru_b_a), row(0.5 * lru_b_x), row(lru_lambda), (0.5 * w_bl).astype(BF16),
        w_out.astype(BF16), row(g_post), (0.5 * w_ple).astype(BF16), (0.5 * w_ple_gate).astype(BF16),
    ]
    scratch = [
        pltpu.VMEM((NSLAB, TM, LANES), F32),
        pltpu.VMEM((TM, D_MODEL), BF16),
        pltpu.VMEM((TM, 2 * NSTATE), F32),
        pltpu.VMEM((TM, 2 * NSTATE), BF16),
        pltpu.VMEM((HALO + TM, LRU_WIDTH), F32),
        pltpu.VMEM((TM, LRU_WIDTH), F32),
        pltpu.VMEM((TM, LRU_WIDTH), F32),
        pltpu.VMEM((NSLAB, TM, LANES), F32),
        pltpu.VMEM((TM, D_MODEL), BF16),
        pltpu.VMEM((TM, D_MODEL), F32),
        pltpu.VMEM((TM, D_MODEL), F32),
        pltpu.VMEM((TM, D_MODEL), F32),
        pltpu.VMEM((TM, LRU_WIDTH), F32),
        pltpu.VMEM((SUBLANES, 2 * NSTATE), F32),
        pltpu.VMEM((SUBLANES, LRU_WIDTH), F32),
        pltpu.VMEM((HALO, LRU_WIDTH), F32),
    ]

    def main_tile(s):
        t = jnp.minimum(s, n_tiles - 1)
        return t // nt, t % nt

    def tail_tile(s):
        t = jnp.maximum(s - 1, 0)
        return t // nt, t % nt

    for layer in range(depth):
        in_specs = [
            pl.BlockSpec((1, TM, D_MODEL), lambda s: main_tile(s) + (0,)),
            pl.BlockSpec((None, 1, TM, PLE_DIM), lambda s, layer=layer: (layer,) + tail_tile(s) + (0,)),
        ] + [_layer_spec(c.shape, layer) for c in consts]
        x = pl.pallas_call(
            functools.partial(_layer_kernel, nt),
            out_shape=jax.ShapeDtypeStruct((B, L, D_MODEL), x.dtype),
            grid=(n_tiles + 1,),
            in_specs=in_specs,
            out_specs=pl.BlockSpec((1, TM, D_MODEL), lambda s: tail_tile(s) + (0,)),
            scratch_shapes=scratch,
            compiler_params=pltpu.CompilerParams(
                dimension_semantics=("arbitrary",),
                vmem_limit_bytes=VMEM_LIMIT_BYTES),
            name="hybrid_layer",
        )(x, p, *consts)
    return x
```
